```python
import math
import jax
import jax.numpy as jnp
from jax import lax
import numpy as np

D_MODEL = 1024
BATCH = 8
SEQ = 4096
DEPTH = 1
DEC_BATCH = 32
DEC_SEQ = 1
PAST_LEN = 16384
PAGE_SIZE = 128

ATT_HEADS = 4
HEAD_DIM = 64
V_DIM = 2 * HEAD_DIM
ATT_WIDTH = ATT_HEADS * V_DIM
QK_WIDTH = 2 * ATT_HEADS * HEAD_DIM
CONV_CH = D_MODEL - ATT_WIDTH
CONV_K = 31
IN_WIDTH = 2 * QK_WIDTH + ATT_WIDTH + 2 * CONV_CH
MIX_WIDTH = ATT_WIDTH + CONV_CH
N_GROUPS = 4
EXPERTS_PER_GROUP = 8
N_EXPERTS = N_GROUPS * EXPERTS_PER_GROUP
TOP_K = 2
D_EXPERT = 256
ROPE_THETA = 10000.0
LN_EPS = 1e-5
ALPHA = (2.0 * DEPTH) ** 0.25
BETA = (8.0 * DEPTH) ** -0.25
Q_BLOCK = 128

kernel_name = "hymba_diffattn_conformerconv_hmoe_step"


def lambda_init(layer):
    return 0.8 - 0.6 * math.exp(-0.3 * layer)


def layer_norm(x, g, b):
    xf = x.astype(jnp.float32)
    mu = jnp.mean(xf, -1, keepdims=True)
    var = jnp.mean(jnp.square(xf - mu), -1, keepdims=True)
    return ((xf - mu) * lax.rsqrt(var + LN_EPS) * g + b).astype(x.dtype)


def rms_norm(x, g):
    xf = x.astype(jnp.float32)
    return (xf * lax.rsqrt(jnp.mean(jnp.square(xf), -1, keepdims=True) + LN_EPS) * g).astype(x.dtype)


def rope(x, pos):
    half = HEAD_DIM // 2
    inv_freq = ROPE_THETA ** (-jnp.arange(half, dtype=jnp.float32) / half)
    ang = pos.astype(jnp.float32)[:, None] * inv_freq[None, :]
    cos = jnp.cos(ang)[:, None, :]
    sin = jnp.sin(ang)[:, None, :]
    xf = x.astype(jnp.float32)
    x1, x2 = xf[..., :half], xf[..., half:]
    return jnp.concatenate([x1 * cos - x2 * sin, x2 * cos + x1 * sin], -1).astype(x.dtype)


def in_project(x, w_in, pos):
    n, s, _ = x.shape
    h = jnp.einsum('bsd,de->bse', x, w_in)
    q, k, v, a, g = jnp.split(h, [QK_WIDTH, 2 * QK_WIDTH, 2 * QK_WIDTH + ATT_WIDTH,
                                  2 * QK_WIDTH + ATT_WIDTH + CONV_CH], axis=-1)
    q = rope(q.reshape(n, s, 2 * ATT_HEADS, HEAD_DIM), pos)
    k = rope(k.reshape(n, s, 2 * ATT_HEADS, HEAD_DIM), pos)
    v = v.reshape(n, s, ATT_HEADS, V_DIM)
    u = a * jax.nn.sigmoid(g)
    return q, k, v, u


def diff_lambda(lq1, lk1, lq2, lk2, lam_init):
    return (jnp.exp(jnp.sum(lq1.astype(jnp.float32) * lk1))
            - jnp.exp(jnp.sum(lq2.astype(jnp.float32) * lk2)) + lam_init)


def diff_weights(s, lam):
    p = jax.nn.softmax(s, axis=-1)
    n, _, q, k = p.shape
    p = p.reshape(n, ATT_HEADS, 2, q, k)
    return p[:, :, 0] - lam * p[:, :, 1]


def attn_prompt(q, k, v, lam):
    n, s = q.shape[:2]
    nb = s // Q_BLOCK
    scale = HEAD_DIM ** -0.5
    qb = q.reshape(n, nb, Q_BLOCK, 2 * ATT_HEADS, HEAD_DIM).transpose(1, 0, 2, 3, 4)
    kpos = jnp.arange(s)

    def block(args):
        qi, i = args
        sc = jnp.einsum('nqhd,nkhd->nhqk', qi, k, preferred_element_type=jnp.float32) * scale
        qpos = i * Q_BLOCK + jnp.arange(Q_BLOCK)
        mask = kpos[None, :] <= qpos[:, None]
        sc = jnp.where(mask[None, None], sc, -jnp.inf)
        a = diff_weights(sc, lam).astype(v.dtype)
        return jnp.einsum('nhqk,nkhe->nqhe', a, v)

    o = lax.map(block, (qb, jnp.arange(nb)))
    return o.transpose(1, 0, 2, 3, 4).reshape(n, s, ATT_HEADS, V_DIM)


def attn_sample(q, k_new, v_new, k_past, v_past, lam):
    scale = HEAD_DIM ** -0.5
    t = q.shape[1]
    p_len = k_past.shape[1]
    s_past = jnp.einsum('nqhd,nkhd->nhqk', q, k_past, preferred_element_type=jnp.float32) * scale
    s_new = jnp.einsum('nqhd,nkhd->nhqk', q, k_new, preferred_element_type=jnp.float32) * scale
    causal = jnp.tril(jnp.ones((t, t), dtype=bool))
    s_new = jnp.where(causal[None, None], s_new, -jnp.inf)
    a = diff_weights(jnp.concatenate([s_past, s_new], -1), lam).astype(v_new.dtype)
    return (jnp.einsum('nhqk,nkhe->nqhe', a[..., :p_len], v_past)
            + jnp.einsum('nhqk,nkhe->nqhe', a[..., p_len:], v_new))


def mixer_out(att, u_ext, lam_init, subln_g, conv_w, conv_b, conv_ln_g, conv_ln_b, w_out):
    n, s = att.shape[:2]
    a = (rms_norm(att, subln_g) * (1.0 - lam_init)).reshape(n, s, ATT_WIDTH)
    c = lax.conv_general_dilated(u_ext, conv_w[:, None, :], (1,), 'VALID',
                                 dimension_numbers=('NWC', 'WIO', 'NWC'),
                                 feature_group_count=CONV_CH) + conv_b
    c = jax.nn.silu(layer_norm(c, conv_ln_g, conv_ln_b))
    return jnp.einsum('bse,ed->bsd', jnp.concatenate([a, c], -1), w_out)


def hier_moe(x, w_r1, b_r1, w_r2, b_r2, w_gate, w_up, w_down):
    t = x.reshape(-1, D_MODEL)
    lg1 = jnp.einsum('td,dg->tg', t, w_r1, preferred_element_type=jnp.float32) + b_r1
    p1 = jax.nn.softmax(lg1, axis=-1)
    _, g_sel = lax.top_k(lg1, 1)
    g_onehot = jax.nn.one_hot(g_sel[:, 0], N_GROUPS, dtype=jnp.float32)
    p_g = jnp.sum(p1 * g_onehot, -1, keepdims=True)
    lg2 = jnp.einsum('td,gde->tge', t, w_r2, preferred_element_type=jnp.float32) + b_r2
    lg2 = jnp.einsum('tg,tge->te', g_onehot, lg2)
    v2, i2 = lax.top_k(lg2, TOP_K)
    w2 = jax.nn.softmax(v2, axis=-1) * p_g
    eid = g_sel * EXPERTS_PER_GROUP + i2
    gates = jnp.sum(jax.nn.one_hot(eid, N_EXPERTS, dtype=jnp.float32) * w2[..., None], axis=1).astype(t.dtype)
    out = jnp.zeros_like(t)
    for e in range(N_EXPERTS):
        h = jax.nn.silu(t @ w_gate[e]) * (t @ w_up[e])
        out = out + gates[:, e:e + 1] * (h @ w_down[e])
    return out.reshape(x.shape)


def post_blocks(x, mix, ln1_g, ln1_b, w_r1, b_r1, w_r2, b_r2, w_gate, w_up, w_down, ln2_g, ln2_b):
    h = layer_norm(ALPHA * x + mix, ln1_g, ln1_b)
    return layer_norm(ALPHA * h + hier_moe(h, w_r1, b_r1, w_r2, b_r2, w_gate, w_up, w_down), ln2_g, ln2_b)


def setup_inputs(seed: int = 0) -> dict:
    key = jax.random.key(seed)
    ks = jax.random.split(key, 32)
    n_pages = PAST_LEN // PAGE_SIZE
    n_used = DEC_BATCH * n_pages
    n_pool = n_used + max(1, n_used // 4)
    page_table = jax.random.permutation(ks[0], n_pool)[:n_used].reshape(DEC_BATCH, n_pages).astype(jnp.int32)

    def nrm(k, shape, scale):
        return jax.random.normal(k, shape, jnp.float32) * scale

    col_scale = jnp.ones((IN_WIDTH,), jnp.float32).at[2 * QK_WIDTH:2 * QK_WIDTH + ATT_WIDTH].set(BETA)
    return {
        'x_prompt': nrm(ks[1], (BATCH, SEQ, D_MODEL), 1.0),
        'x_sample': nrm(ks[2], (DEC_BATCH, DEC_SEQ, D_MODEL), 1.0),
        'cache_k': nrm(ks[3], (DEPTH, n_pool, PAGE_SIZE, 2 * ATT_HEADS, HEAD_DIM), 1.0),
        'cache_v': nrm(ks[4], (DEPTH, n_pool, PAGE_SIZE, ATT_HEADS, V_DIM), 1.0),
        'state_conv': nrm(ks[5], (DEPTH, DEC_BATCH, CONV_K - 1, CONV_CH), 0.5),
        'page_table': page_table,
        'w_in': nrm(ks[6], (DEPTH, D_MODEL, IN_WIDTH), D_MODEL ** -0.5) * col_scale,
        'lambda_q1': nrm(ks[7], (DEPTH, HEAD_DIM), 0.1),
        'lambda_k1': nrm(ks[8], (DEPTH, HEAD_DIM), 0.1),
        'lambda_q2': nrm(ks[9], (DEPTH, HEAD_DIM), 0.1),
        'lambda_k2': nrm(ks[10], (DEPTH, HEAD_DIM), 0.1),
        'subln_g': 1.0 + nrm(ks[11], (DEPTH, V_DIM), 0.02),
        'conv_w': nrm(ks[12], (DEPTH, CONV_K, CONV_CH), CONV_K ** -0.5),
        'conv_b': nrm(ks[13], (DEPTH, CONV_CH), 0.02),
        'conv_ln_g': 1.0 + nrm(ks[14], (DEPTH, CONV_CH), 0.02),
        'conv_ln_b': nrm(ks[15], (DEPTH, CONV_CH), 0.02),
        'w_out': nrm(ks[16], (DEPTH, MIX_WIDTH, D_MODEL), MIX_WIDTH ** -0.5) * BETA,
        'ln1_g': 1.0 + nrm(ks[17], (DEPTH, D_MODEL), 0.02),
        'ln1_b': nrm(ks[18], (DEPTH, D_MODEL), 0.02),
        'w_r1': nrm(ks[19], (DEPTH, D_MODEL, N_GROUPS), D_MODEL ** -0.5),
        'b_r1': nrm(ks[20], (DEPTH, N_GROUPS), 0.01),
        'w_r2': nrm(ks[21], (DEPTH, N_GROUPS, D_MODEL, EXPERTS_PER_GROUP), D_MODEL ** -0.5),
        'b_r2': nrm(ks[22], (DEPTH, N_GROUPS, EXPERTS_PER_GROUP), 0.01),
        'w_gate': nrm(ks[23], (DEPTH, N_EXPERTS, D_MODEL, D_EXPERT), D_MODEL ** -0.5) * BETA,
        'w_up': nrm(ks[24], (DEPTH, N_EXPERTS, D_MODEL, D_EXPERT), D_MODEL ** -0.5) * BETA,
        'w_down': nrm(ks[25], (DEPTH, N_EXPERTS, D_EXPERT, D_MODEL), D_EXPERT ** -0.5) * BETA,
        'ln2_g': 1.0 + nrm(ks[26], (DEPTH, D_MODEL), 0.02),
        'ln2_b': nrm(ks[27], (DEPTH, D_MODEL), 0.02),
    }


def reference(x_prompt, x_sample, cache_k, cache_v, state_conv, page_table, w_in, lambda_q1, lambda_k1,
              lambda_q2, lambda_k2, subln_g, conv_w, conv_b, conv_ln_g, conv_ln_b, w_out, ln1_g, ln1_b,
              w_r1, b_r1, w_r2, b_r2, w_gate, w_up, w_down, ln2_g, ln2_b):
    pos_p = jnp.arange(x_prompt.shape[1])
    pos_s = PAST_LEN + jnp.arange(x_sample.shape[1])
    n_s = x_sample.shape[0]
    past = page_table.shape[1] * PAGE_SIZE
    hp, hs = x_prompt, x_sample
    kp_l, vp_l, cp_l, ks_l, vs_l, cs_l = [], [], [], [], [], []
    for l in range(DEPTH):
        lam_init = lambda_init(l)
        lam = diff_lambda(lambda_q1[l], lambda_k1[l], lambda_q2[l], lambda_k2[l], lam_init)
        conv_args = (lam_init, subln_g[l], conv_w[l], conv_b[l], conv_ln_g[l], conv_ln_b[l], w_out[l])
        post_args = (ln1_g[l], ln1_b[l], w_r1[l], b_r1[l], w_r2[l], b_r2[l],
                     w_gate[l], w_up[l], w_down[l], ln2_g[l], ln2_b[l])
        q, k, v, u = in_project(hp, w_in[l], pos_p)
        att = attn_prompt(q, k, v, lam)
        u_ext = jnp.pad(u, ((0, 0), (CONV_K - 1, 0), (0, 0)))
        mix = mixer_out(att, u_ext, *conv_args)
        kp_l.append(k)
        vp_l.append(v)
        cp_l.append(u_ext[:, -(CONV_K - 1):])
        hp = post_blocks(hp, mix, *post_args)
        q, k, v, u = in_project(hs, w_in[l], pos_s)
        k_past = cache_k[l][page_table].reshape(n_s, past, 2 * ATT_HEADS, HEAD_DIM)
        v_past = cache_v[l][page_table].reshape(n_s, past, ATT_HEADS, V_DIM)
        att = attn_sample(q, k, v, k_past, v_past, lam)
        u_ext = jnp.concatenate([state_conv[l], u], axis=1)
        mix = mixer_out(att, u_ext, *conv_args)
        ks_l.append(k)
        vs_l.append(v)
        cs_l.append(u_ext[:, -(CONV_K - 1):])
        hs = post_blocks(hs, mix, *post_args)
    return (hp, hs, jnp.stack(kp_l), jnp.stack(vp_l), jnp.stack(cp_l),
            jnp.stack(ks_l), jnp.stack(vs_l), jnp.stack(cs_l))
```

```python
import functools
import math

import jax
import jax.numpy as jnp
from jax import lax
from jax.experimental import pallas as pl
from jax.experimental.pallas import tpu as pltpu

D_MODEL = 1024
PAST_LEN = 16384
PAGE_SIZE = 128
ATT_HEADS = 4
HEAD_DIM = 64
V_DIM = 2 * HEAD_DIM
ATT_WIDTH = ATT_HEADS * V_DIM
QK_WIDTH = 2 * ATT_HEADS * HEAD_DIM
CONV_CH = D_MODEL - ATT_WIDTH
CONV_K = 31
IN_WIDTH = 2 * QK_WIDTH + ATT_WIDTH + 2 * CONV_CH
N_GROUPS = 4
EXPERTS_PER_GROUP = 8
N_EXPERTS = N_GROUPS * EXPERTS_PER_GROUP
D_EXPERT = 256
ROPE_THETA = 10000.0
LN_EPS = 1e-5

LANES = 128
SUBLANES = 8
HALO = 32
PAIRS_PER_GROUP = EXPERTS_PER_GROUP * (EXPERTS_PER_GROUP - 1) // 2
N_BUCKETS = N_GROUPS * PAIRS_PER_GROUP
ROUTE_COLS = N_GROUPS + N_EXPERTS
MOE_TILE = 128
DECODE_ROWS = 16
VMEM_LIMIT = 56 * 1024 * 1024

F32 = jnp.float32
BF16 = jnp.bfloat16


def _lambda_init(layer):
    return 0.8 - 0.6 * math.exp(-0.3 * layer)


def _layer_norm(x, g, b):
    mu = jnp.mean(x, axis=-1, keepdims=True)
    xc = x - mu
    var = jnp.mean(xc * xc, axis=-1, keepdims=True)
    return xc * lax.rsqrt(var + LN_EPS) * g + b


def _sigmoid(x):
    return 1.0 / (1.0 + jnp.exp(-x))


def _rope_tables(pos):
    half = HEAD_DIM // 2
    inv_freq = ROPE_THETA ** (-jnp.arange(half, dtype=F32) / half)
    ang = pos.astype(F32)[:, None] * inv_freq[None, :]
    cos, sin = jnp.cos(ang), jnp.sin(ang)
    zero = jnp.zeros_like(sin)
    cos_t = jnp.tile(cos, (1, LANES // half))
    sin_lo = jnp.tile(jnp.concatenate([-sin, zero], axis=1), (1, LANES // HEAD_DIM))
    sin_hi = jnp.tile(jnp.concatenate([zero, sin], axis=1), (1, LANES // HEAD_DIM))
    return cos_t, sin_lo, sin_hi


def _in_proj_body(x_ref, w_ref, cos_ref, slo_ref, shi_ref,
                  q_ref, k_ref, v_ref, u_ref, kb_ref, vb_ref):
    xb = x_ref[...].astype(BF16)
    cos, slo, shi = cos_ref[...], slo_ref[...], shi_ref[...]

    def seg(c0, width):
        return jnp.dot(xb, w_ref[:, c0:c0 + width], preferred_element_type=F32)

    def rope(h, j):
        xj = h[:, j * LANES:(j + 1) * LANES]
        return (xj * cos + pltpu.roll(xj, LANES - HEAD_DIM // 2, 1) * slo
                + pltpu.roll(xj, HEAD_DIM // 2, 1) * shi)

    hq = seg(0, QK_WIDTH)
    for j in range(QK_WIDTH // LANES):
        sl = slice(j * LANES, (j + 1) * LANES)
        q_ref[:, sl] = (rope(hq, j) * (HEAD_DIM ** -0.5)).astype(BF16)
    hk = seg(QK_WIDTH, QK_WIDTH)
    for j in range(QK_WIDTH // LANES):
        sl = slice(j * LANES, (j + 1) * LANES)
        kr = rope(hk, j)
        k_ref[:, sl] = kr
        kb_ref[:, sl] = kr.astype(BF16)
    hv = seg(2 * QK_WIDTH, ATT_WIDTH)
    v_ref[...] = hv
    vb_ref[...] = hv.astype(BF16)
    ha = seg(2 * QK_WIDTH + ATT_WIDTH, CONV_CH)
    hg = seg(2 * QK_WIDTH + ATT_WIDTH + CONV_CH, CONV_CH)
    u_ref[...] = ha * _sigmoid(hg)


def _in_proj(x2d, w_bf, pos, tm):
    t = x2d.shape[0]
    cos_t, slo_t, shi_t = _rope_tables(pos)
    tab_blocks = cos_t.shape[0] // tm
    row = lambda i: (i, 0)
    tab = lambda i: (i % tab_blocks, 0)
    outs = pl.pallas_call(
        _in_proj_body,
        grid=(t // tm,),
        in_specs=[
            pl.BlockSpec((tm, D_MODEL), row),
            pl.BlockSpec((D_MODEL, IN_WIDTH), lambda i: (0, 0)),
            pl.BlockSpec((tm, LANES), tab),
            pl.BlockSpec((tm, LANES), tab),
            pl.BlockSpec((tm, LANES), tab),
        ],
        out_specs=[
            pl.BlockSpec((tm, QK_WIDTH), row),
            pl.BlockSpec((tm, QK_WIDTH), row),
            pl.BlockSpec((tm, ATT_WIDTH), row),
            pl.BlockSpec((tm, CONV_CH), row),
            pl.BlockSpec((tm, QK_WIDTH), row),
            pl.BlockSpec((tm, ATT_WIDTH), row),
        ],
        out_shape=[
            jax.ShapeDtypeStruct((t, QK_WIDTH), BF16),
            jax.ShapeDtypeStruct((t, QK_WIDTH), F32),
            jax.ShapeDtypeStruct((t, ATT_WIDTH), F32),
            jax.ShapeDtypeStruct((t, CONV_CH), F32),
            jax.ShapeDtypeStruct((t, QK_WIDTH), BF16),
            jax.ShapeDtypeStruct((t, ATT_WIDTH), BF16),
        ],
        compiler_params=pltpu.CompilerParams(
            dimension_semantics=("parallel",), vmem_limit_bytes=VMEM_LIMIT),
        name="in_proj",
    )(x2d, w_bf, cos_t, slo_t, shi_t)
    return outs


def _sub_ln(o, g, lam_init):
    ms = jnp.mean(o * o, axis=-1, keepdims=True)
    return o * lax.rsqrt(ms + LN_EPS) * g * (1.0 - lam_init)


def _attn_body(lam_ref, q_ref, k_ref, v_ref, g_ref, o_ref, m_sc, l_sc, acc_sc, *, bq, lam_init):
    i = pl.program_id(2)
    q = q_ref[0]
    lane = lax.broadcasted_iota(jnp.int32, q.shape, 1)
    zero = jnp.zeros_like(q)
    qq = jnp.concatenate([jnp.where(lane < HEAD_DIM, q, zero),
                          jnp.where(lane >= HEAD_DIM, q, zero)], axis=0)
    m_sc[...] = jnp.full(m_sc.shape, -jnp.inf, F32)
    l_sc[...] = jnp.zeros(l_sc.shape, F32)
    acc_sc[...] = jnp.zeros(acc_sc.shape, F32)

    def step(j, diagonal):
        start = pl.multiple_of(j * bq, bq)
        k = k_ref[0, pl.ds(start, bq), :]
        v = v_ref[0, pl.ds(start, bq), :]
        s = lax.dot_general(qq, k, (((1,), (1,)), ((), ())), preferred_element_type=F32)
        if diagonal:
            r = lax.broadcasted_iota(jnp.int32, s.shape, 0)
            c = lax.broadcasted_iota(jnp.int32, s.shape, 1)
            r = jnp.where(r >= bq, r - bq, r)
            s = jnp.where(c <= r, s, -jnp.inf)
        m_prev = m_sc[...]
        m_new = jnp.maximum(m_prev, jnp.max(s, axis=1, keepdims=True))
        alpha = jnp.exp(m_prev - m_new)
        p = jnp.exp(s - m_new[:, :1])
        l_sc[...] = alpha * l_sc[...] + jnp.sum(p, axis=1, keepdims=True)
        acc_sc[...] = alpha * acc_sc[...] + jnp.dot(p.astype(BF16), v, preferred_element_type=F32)
        m_sc[...] = m_new

    def full_step(j, carry):
        step(j, False)
        return carry

    lax.fori_loop(0, i, full_step, 0)
    step(i, True)

    lam = lam_ref[0]
    o = acc_sc[...] / l_sc[...]
    o = o[:bq] - lam * o[bq:]
    o_ref[0] = _sub_ln(o, g_ref[...], lam_init).astype(BF16)


def _attn_prompt(qb, kb, vb, lam, subln_g, lam_init, bq):
    n, s, _ = qb.shape
    body = functools.partial(_attn_body, bq=bq, lam_init=lam_init)
    return pl.pallas_call(
        body,
        grid=(n, ATT_HEADS, s // bq),
        in_specs=[
            pl.BlockSpec(memory_space=pltpu.SMEM),
            pl.BlockSpec((1, bq, V_DIM), lambda b, h, i: (b, i, h)),
            pl.BlockSpec((1, s, V_DIM), lambda b, h, i: (b, 0, h)),
            pl.BlockSpec((1, s, V_DIM), lambda b, h, i: (b, 0, h)),
            pl.BlockSpec((1, V_DIM), lambda b, h, i: (0, 0)),
        ],
        out_specs=pl.BlockSpec((1, bq, V_DIM), lambda b, h, i: (b, i, h)),
        scratch_shapes=[
            pltpu.VMEM((2 * bq, V_DIM), F32),
            pltpu.VMEM((2 * bq, V_DIM), F32),
            pltpu.VMEM((2 * bq, V_DIM), F32),
        ],
        out_shape=jax.ShapeDtypeStruct((n, s, ATT_WIDTH), BF16),
        compiler_params=pltpu.CompilerParams(
            dimension_semantics=("parallel", "parallel", "arbitrary"), vmem_limit_bytes=VMEM_LIMIT),
        name="attn_prompt",
    )(lam, qb, kb, vb, subln_g)


def _decode_body(pt_ref, lam_ref, q_ref, kn_ref, vn_ref, g_ref, *rest, pages, lam_init):
    k_refs = rest[:pages]
    v_refs = rest[pages:2 * pages]
    o_ref, qbd_sc, m_sc, l_sc, acc_sc = rest[2 * pages:]
    n_comp = DECODE_ROWS
    j = pl.program_id(1)

    @pl.when(j == 0)
    def _():
        q = jnp.broadcast_to(q_ref[0].astype(F32), (n_comp, QK_WIDTH))
        row = lax.broadcasted_iota(jnp.int32, q.shape, 0)
        lane = lax.broadcasted_iota(jnp.int32, q.shape, 1)
        qbd_sc[...] = jnp.where((lane >= row * HEAD_DIM) & (lane < (row + 1) * HEAD_DIM), q, 0.0)
        m_sc[...] = jnp.full(m_sc.shape, -jnp.inf, F32)
        l_sc[...] = jnp.zeros(l_sc.shape, F32)
        acc_sc[...] = jnp.zeros(acc_sc.shape, F32)

    qbd_f = qbd_sc[...]
    qbd = qbd_f.astype(BF16)
    s = jnp.concatenate(
        [lax.dot_general(qbd, kr[0].astype(BF16), (((1,), (1,)), ((), ())),
                         preferred_element_type=F32) for kr in k_refs], axis=1)
    m_prev = m_sc[...]
    m_new = jnp.maximum(m_prev, jnp.max(s, axis=1, keepdims=True))
    alpha = jnp.exp(m_prev - m_new)
    p = jnp.exp(s - m_new[:, :1])
    l_sc[...] = alpha * l_sc[...] + jnp.sum(p, axis=1, keepdims=True)
    pb = p.astype(BF16)
    pv = jnp.zeros(acc_sc.shape, F32)
    for idx, vr in enumerate(v_refs):
        pv = pv + jnp.dot(pb[:, idx * PAGE_SIZE:(idx + 1) * PAGE_SIZE], vr[0].astype(BF16),
                          preferred_element_type=F32)
    acc_sc[...] = alpha[:, :1] * acc_sc[...] + pv
    m_sc[...] = m_new

    @pl.when(j == pl.num_programs(1) - 1)
    def _():
        kn = kn_ref[0].astype(F32)
        vn = vn_ref[0].astype(F32)
        s_self = jnp.sum(qbd_f * kn, axis=1, keepdims=True)
        m_last = m_sc[...]
        m_fin = jnp.maximum(m_last, s_self)
        a_fin = jnp.exp(m_last - m_fin)
        p_self = jnp.exp(s_self - m_fin[:, :1])
        l_fin = l_sc[...] * a_fin + p_self
        acc = acc_sc[...] * a_fin[:, :1] + p_self.astype(BF16).astype(F32) * vn
        o = acc / l_fin[:, :1]
        row = lax.broadcasted_iota(jnp.int32, o.shape, 0)
        head = lax.broadcasted_iota(jnp.int32, o.shape, 1) // V_DIM
        lam = lam_ref[0]
        coef = jnp.where(row == 2 * head, 1.0, jnp.where(row == 2 * head + 1, -lam, 0.0))
        oc = jnp.sum(o * coef, axis=0, keepdims=True)
        g = g_ref[...]
        for h in range(ATT_HEADS):
            sl = slice(h * V_DIM, (h + 1) * V_DIM)
            o_ref[0, :, sl] = _sub_ln(oc[:, sl], g, lam_init).astype(BF16)


def _attn_decode(page_table, qb, kb_new, vb_new, cache_k, cache_v, lam, subln_g, lam_init, pages):
    n, n_pages = page_table.shape
    n_pool = cache_k.shape[0]
    ck = cache_k.reshape(n_pool, PAGE_SIZE, QK_WIDTH)
    cv = cache_v.reshape(n_pool, PAGE_SIZE, ATT_WIDTH)
    pt_flat = page_table.reshape(-1)

    def page_map(idx):
        return lambda b, j, pt: (pt[b * n_pages + j * pages + idx], 0, 0)

    row = lambda b, j, pt: (b, 0, 0)
    body = functools.partial(_decode_body, pages=pages, lam_init=lam_init)
    n_comp = DECODE_ROWS
    return pl.pallas_call(
        body,
        grid_spec=pltpu.PrefetchScalarGridSpec(
            num_scalar_prefetch=1,
            grid=(n, n_pages // pages),
            in_specs=(
                [pl.BlockSpec(memory_space=pltpu.SMEM),
                 pl.BlockSpec((1, 1, QK_WIDTH), row),
                 pl.BlockSpec((1, 1, QK_WIDTH), row),
                 pl.BlockSpec((1, 1, ATT_WIDTH), row),
                 pl.BlockSpec((1, V_DIM), lambda b, j, pt: (0, 0))]
                + [pl.BlockSpec((1, PAGE_SIZE, QK_WIDTH), page_map(i)) for i in range(pages)]
                + [pl.BlockSpec((1, PAGE_SIZE, ATT_WIDTH), page_map(i)) for i in range(pages)]),
            out_specs=pl.BlockSpec((1, 1, ATT_WIDTH), row),
            scratch_shapes=[
                pltpu.VMEM((n_comp, QK_WIDTH), F32),
                pltpu.VMEM((n_comp, LANES), F32),
                pltpu.VMEM((n_comp, LANES), F32),
                pltpu.VMEM((n_comp, ATT_WIDTH), F32),
            ],
        ),
        out_shape=jax.ShapeDtypeStruct((n, 1, ATT_WIDTH), BF16),
        compiler_params=pltpu.CompilerParams(
            dimension_semantics=("parallel", "arbitrary"), vmem_limit_bytes=VMEM_LIMIT),
        name="attn_decode",
    )(pt_flat, lam, qb.reshape(n, 1, QK_WIDTH), kb_new.reshape(n, 1, QK_WIDTH),
      vb_new.reshape(n, 1, ATT_WIDTH), subln_g, *([ck] * pages), *([cv] * pages))


def _route(logits):
    lane = lax.broadcasted_iota(jnp.int32, logits.shape, 1)
    lane_f = lane.astype(F32)
    neg = -jnp.inf
    big = float(LANES)

    def first_argmax(mask, vmax):
        return jnp.min(jnp.where(mask & (logits == vmax), lane_f, big), axis=1, keepdims=True)

    g_mask = lane < N_GROUPS
    g_max = jnp.max(jnp.where(g_mask, logits, neg), axis=1, keepdims=True)
    g_sel = first_argmax(g_mask, g_max)
    p_g = 1.0 / jnp.sum(jnp.where(g_mask, jnp.exp(logits - g_max), 0.0), axis=1, keepdims=True)
    lo = N_GROUPS + EXPERTS_PER_GROUP * g_sel
    e_mask = (lane_f >= lo) & (lane_f < lo + EXPERTS_PER_GROUP)
    v1 = jnp.max(jnp.where(e_mask, logits, neg), axis=1, keepdims=True)
    i1 = first_argmax(e_mask, v1)
    e_mask2 = e_mask & (lane_f != i1)
    v2 = jnp.max(jnp.where(e_mask2, logits, neg), axis=1, keepdims=True)
    i2 = first_argmax(e_mask2, v2)
    t = jnp.exp(v2 - v1)
    den = 1.0 + t
    w_first = (1.0 / den) * p_g
    w_second = (t / den) * p_g
    a = i1 - lo
    b = i2 - lo
    first_low = a < b
    e_lo = jnp.minimum(a, b)
    e_hi = jnp.maximum(a, b)
    pair = e_lo * (EXPERTS_PER_GROUP - 1) - e_lo * (e_lo - 1.0) * 0.5 + (e_hi - e_lo - 1.0)
    bucket = g_sel * PAIRS_PER_GROUP + pair
    w_lo = jnp.where(first_low, w_first, w_second)
    w_hi = jnp.where(first_low, w_second, w_first)
    return jnp.where(lane == 0, bucket, jnp.where(lane == 1, w_lo, jnp.where(lane == 2, w_hi, 0.0)))


def _mix_ln_route(att_b, c, x, wout_ref, ln1g_ref, ln1b_ref, wrt_ref, brt_ref, alpha):
    cb = (c * _sigmoid(c)).astype(BF16)
    mix = (jnp.dot(att_b, wout_ref[:ATT_WIDTH, :], preferred_element_type=F32)
           + jnp.dot(cb, wout_ref[ATT_WIDTH:, :], preferred_element_type=F32))
    h = _layer_norm(alpha * x + mix, ln1g_ref[...], ln1b_ref[...])
    logits = jnp.dot(h.astype(BF16), wrt_ref[...], preferred_element_type=F32) + brt_ref[...]
    return h, _route(logits)


def _store_token_major(dst_ref, val):
    rows = val.shape[0]
    for c in range(D_MODEL // LANES):
        dst_ref[pl.ds(c, rows, stride=SUBLANES), :] = val[:, c * LANES:(c + 1) * LANES]


def _load_token_major(src_ref, rows):
    return jnp.concatenate([src_ref[pl.ds(c, rows, stride=SUBLANES), :]
                            for c in range(D_MODEL // LANES)], axis=1)


def _post_prompt_body(att_ref, u_ref, halo_ref, x_ref, cw_ref, cb_ref, cg_ref, cbeta_ref,
                      wout_ref, ln1g_ref, ln1b_ref, wrt_ref, brt_ref,
                      h_ref, info_ref, ext_sc, c_sc, *, ts, alpha, row_chunk):
    t = pl.program_id(1)

    @pl.when(t == 0)
    def _():
        ext_sc[0:HALO, :] = jnp.zeros((HALO, CONV_CH), F32)

    @pl.when(t > 0)
    def _():
        ext_sc[0:HALO, :] = halo_ref[0]

    ext_sc[HALO:HALO + ts, :] = u_ref[0]
    first = HALO - (CONV_K - 1)
    bias = cb_ref[...]
    for r0 in range(0, ts, row_chunk):
        acc = jnp.broadcast_to(bias, (row_chunk, CONV_CH))
        for kk in range(CONV_K):
            acc = acc + ext_sc[r0 + first + kk:r0 + first + kk + row_chunk, :] * cw_ref[kk:kk + 1, :]
        c_sc[r0:r0 + row_chunk, :] = _layer_norm(acc, cg_ref[...], cbeta_ref[...])
    h, info = _mix_ln_route(att_ref[0], c_sc[...], x_ref[0], wout_ref, ln1g_ref, ln1b_ref,
                            wrt_ref, brt_ref, alpha)
    _store_token_major(h_ref, h)
    info_ref[...] = info


def _post_prompt(att, u, x, cw, cb, cg, cbeta, wout_b, ln1g, ln1b, wrt_b, brt, alpha, ts):
    n, s, _ = x.shape
    t_blocks = s // ts
    body = functools.partial(_post_prompt_body, ts=ts, alpha=alpha, row_chunk=64)
    const = lambda b, t: (0, 0)
    return pl.pallas_call(
        body,
        grid=(n, t_blocks),
        in_specs=[
            pl.BlockSpec((1, ts, ATT_WIDTH), lambda b, t: (b, t, 0)),
            pl.BlockSpec((1, ts, CONV_CH), lambda b, t: (b, t, 0)),
            pl.BlockSpec((1, HALO, CONV_CH), lambda b, t: (b, jnp.maximum(t * (ts // HALO) - 1, 0), 0)),
            pl.BlockSpec((1, ts, D_MODEL), lambda b, t: (b, t, 0)),
            pl.BlockSpec((CONV_K, CONV_CH), const),
            pl.BlockSpec((1, CONV_CH), const),
            pl.BlockSpec((1, CONV_CH), const),
            pl.BlockSpec((1, CONV_CH), const),
            pl.BlockSpec((D_MODEL, D_MODEL), const),
            pl.BlockSpec((1, D_MODEL), const),
            pl.BlockSpec((1, D_MODEL), const),
            pl.BlockSpec((D_MODEL, LANES), const),
            pl.BlockSpec((1, LANES), const),
        ],
        out_specs=[
            pl.BlockSpec((ts * SUBLANES, LANES), lambda b, t: (b * t_blocks + t, 0)),
            pl.BlockSpec((ts, LANES), lambda b, t: (b * t_blocks + t, 0)),
        ],
        out_shape=[
            jax.ShapeDtypeStruct((n * s * SUBLANES, LANES), F32),
            jax.ShapeDtypeStruct((n * s, LANES), F32),
        ],
        scratch_shapes=[
            pltpu.VMEM((HALO + ts, CONV_CH), F32),
            pltpu.VMEM((ts, CONV_CH), F32),
        ],
        compiler_params=pltpu.CompilerParams(
            dimension_semantics=("parallel", "arbitrary"), vmem_limit_bytes=VMEM_LIMIT),
        name="post_prompt",
    )(att, u, u, x, cw, cb, cg, cbeta, wout_b, ln1g, ln1b, wrt_b, brt)


def _post_sample_body(att_ref, st_ref, u_ref, x_ref, cw_ref, cb_ref, cg_ref, cbeta_ref,
                      wout_ref, ln1g_ref, ln1b_ref, wrt_ref, brt_ref, h_ref, info_ref, *, alpha):
    acc = cb_ref[...] + u_ref[...] * cw_ref[CONV_K - 1:CONV_K, :]
    for kk in range(CONV_K - 1):
        acc = acc + st_ref[kk] * cw_ref[kk:kk + 1, :]
    c = _layer_norm(acc, cg_ref[...], cbeta_ref[...])
    h, info = _mix_ln_route(att_ref[...], c, x_ref[...], wout_ref, ln1g_ref, ln1b_ref,
                            wrt_ref, brt_ref, alpha)
    _store_token_major(h_ref, h)
    info_ref[...] = info


def _post_sample(att, state_t, u, x, cw, cb, cg, cbeta, wout_b, ln1g, ln1b, wrt_b, brt, alpha):
    n = x.shape[0]
    body = functools.partial(_post_sample_body, alpha=alpha)
    return pl.pallas_call(
        body,
        out_shape=[
            jax.ShapeDtypeStruct((n * SUBLANES, LANES), F32),
            jax.ShapeDtypeStruct((n, LANES), F32),
        ],
        compiler_params=pltpu.CompilerParams(vmem_limit_bytes=VMEM_LIMIT),
        name="post_sample",
    )(att, state_t, u, x, cw, cb, cg, cbeta, wout_b, ln1g, ln1b, wrt_b, brt)


def _dispatch(info, tm):
    t = info.shape[0]
    n_tiles = t // tm + min(N_BUCKETS, t) + 1
    bucket = info[:, 0].astype(jnp.int32)
    order = jnp.argsort(bucket, stable=True).astype(jnp.int32)
    sorted_bucket = bucket[order]
    counts = jnp.zeros((N_BUCKETS,), jnp.int32).at[bucket].add(1)
    tiles_per = (counts + tm - 1) // tm
    tile_end = jnp.cumsum(tiles_per)
    tile_start = tile_end - tiles_per
    row_start = jnp.cumsum(counts) - counts
    rank = jnp.arange(t, dtype=jnp.int32) - row_start[sorted_bucket]
    slot = tile_start[sorted_bucket] * tm + rank
    tok_of_slot = jnp.zeros((n_tiles * tm,), jnp.int32).at[slot].set(order)
    w_sorted = info[order, 1:3]
    w_of_slot = jnp.zeros((n_tiles * tm, LANES), F32).at[slot, 0:2].set(w_sorted)
    slot_of_tok = jnp.zeros((t,), jnp.int32).at[order].set(slot)
    tile_ids = jnp.arange(n_tiles, dtype=jnp.int32)
    tile_bucket = jnp.searchsorted(tile_end, tile_ids, side="right").astype(jnp.int32)
    used = tile_bucket < N_BUCKETS
    tb = jnp.minimum(tile_bucket, N_BUCKETS - 1)
    n_valid = jnp.where(used, jnp.clip(counts[tb] - (tile_ids - tile_start[tb]) * tm, 0, tm), 0)
    last_bucket = tb[jnp.maximum(tile_end[-1] - 1, 0)]
    tb = jnp.where(used, tb, last_bucket)
    pair = tb % PAIRS_PER_GROUP
    lo_tab, hi_tab = [], []
    for lo in range(EXPERTS_PER_GROUP):
        for hi in range(lo + 1, EXPERTS_PER_GROUP):
            lo_tab.append(lo)
            hi_tab.append(hi)
    e_lo = jnp.asarray(lo_tab, jnp.int32)[pair]
    e_hi = jnp.asarray(hi_tab, jnp.int32)[pair]
    return dict(tok_of_slot=tok_of_slot, w_of_slot=w_of_slot, slot_of_tok=slot_of_tok,
                tile_group=(tb // PAIRS_PER_GROUP).astype(jnp.int32), e_lo=e_lo, e_hi=e_hi,
                n_valid=n_valid.astype(jnp.int32), n_tiles=n_tiles)


def _moe_body(grp_ref, elo_ref, ehi_ref, nv_ref, tok_ref,
              h_hbm, w_ref, wg_ref, wu_ref, wd_ref, g2_ref, b2_ref,
              y_ref, xbuf, sem, *, tm, alpha):
    t = pl.program_id(0)
    nv = nv_ref[t]

    @pl.when(nv == 0)
    def _():
        y_ref[...] = jnp.zeros(y_ref.shape, F32)

    @pl.when(nv > 0)
    def _():
        base = t * tm

        def row_copy(r, tok):
            return pltpu.make_async_copy(
                h_hbm.at[tok], xbuf.at[pl.ds(pl.multiple_of(r * SUBLANES, SUBLANES), SUBLANES)], sem)

        def issue(r, carry):
            row_copy(r, tok_ref[base + r]).start()
            return carry

        def wait(r, carry):
            row_copy(r, 0).wait()
            return carry

        lax.fori_loop(0, tm, issue, 0)
        lax.fori_loop(0, tm, wait, 0)
        x = _load_token_major(xbuf, tm)
        xb = x.astype(BF16)

        def expert(e):
            gate = jnp.dot(xb, wg_ref[0, e], preferred_element_type=F32)
            up = jnp.dot(xb, wu_ref[0, e], preferred_element_type=F32)
            hidden = (gate * _sigmoid(gate) * up).astype(BF16)
            return jnp.dot(hidden, wd_ref[0, e], preferred_element_type=F32)

        w = w_ref[...]
        moe = w[:, 0:1] * expert(elo_ref[t]) + w[:, 1:2] * expert(ehi_ref[t])
        y = _layer_norm(alpha * x + moe, g2_ref[...], b2_ref[...])
        _store_token_major(y_ref, y)


def _moe(h_tok, disp, wg_b, wu_b, wd_b, ln2g, ln2b, alpha, tm):
    n_tiles = disp["n_tiles"]
    body = functools.partial(_moe_body, tm=tm, alpha=alpha)
    wmap = lambda t, grp, elo, ehi, nv, tok: (grp[t], 0, 0, 0)
    const = lambda t, grp, elo, ehi, nv, tok: (0, 0)
    rows = lambda t, grp, elo, ehi, nv, tok: (t, 0)
    return pl.pallas_call(
        body,
        grid_spec=pltpu.PrefetchScalarGridSpec(
            num_scalar_prefetch=5,
            grid=(n_tiles,),
            in_specs=[
                pl.BlockSpec(memory_space=pl.ANY),
                pl.BlockSpec((tm, LANES), rows),
                pl.BlockSpec((1, EXPERTS_PER_GROUP, D_MODEL, D_EXPERT), wmap),
                pl.BlockSpec((1, EXPERTS_PER_GROUP, D_MODEL, D_EXPERT), wmap),
                pl.BlockSpec((1, EXPERTS_PER_GROUP, D_EXPERT, D_MODEL), wmap),
                pl.BlockSpec((1, D_MODEL), const),
                pl.BlockSpec((1, D_MODEL), const),
            ],
            out_specs=pl.BlockSpec((tm * SUBLANES, LANES), rows),
            scratch_shapes=[
                pltpu.VMEM((tm * SUBLANES, LANES), F32),
                pltpu.SemaphoreType.DMA(()),
            ],
        ),
        out_shape=jax.ShapeDtypeStruct((n_tiles * tm * SUBLANES, LANES), F32),
        compiler_params=pltpu.CompilerParams(
            dimension_semantics=("arbitrary",), vmem_limit_bytes=VMEM_LIMIT),
        name="moe",
    )(disp["tile_group"], disp["e_lo"], disp["e_hi"], disp["n_valid"], disp["tok_of_slot"],
      h_tok, disp["w_of_slot"], wg_b, wu_b, wd_b, ln2g, ln2b)


def _unpermute_body(slot_ref, y_hbm, o_ref, buf, sem, *, ts):
    base = pl.program_id(0) * ts

    def row_copy(r, slot):
        return pltpu.make_async_copy(
            y_hbm.at[slot], buf.at[pl.ds(pl.multiple_of(r * SUBLANES, SUBLANES), SUBLANES)], sem)

    def issue(r, carry):
        row_copy(r, slot_ref[base + r]).start()
        return carry

    def wait(r, carry):
        row_copy(r, 0).wait()
        return carry

    lax.fori_loop(0, ts, issue, 0)
    lax.fori_loop(0, ts, wait, 0)
    for c in range(D_MODEL // LANES):
        o_ref[:, c * LANES:(c + 1) * LANES] = buf[pl.ds(c, ts, stride=SUBLANES), :]


def _unpermute(y_slots, slot_of_tok, ts):
    t = slot_of_tok.shape[0]
    body = functools.partial(_unpermute_body, ts=ts)
    return pl.pallas_call(
        body,
        grid_spec=pltpu.PrefetchScalarGridSpec(
            num_scalar_prefetch=1,
            grid=(t // ts,),
            in_specs=[pl.BlockSpec(memory_space=pl.ANY)],
            out_specs=pl.BlockSpec((ts, D_MODEL), lambda i, slot: (i, 0)),
            scratch_shapes=[
                pltpu.VMEM((ts * SUBLANES, LANES), F32),
                pltpu.SemaphoreType.DMA(()),
            ],
        ),
        out_shape=jax.ShapeDtypeStruct((t, D_MODEL), F32),
        compiler_params=pltpu.CompilerParams(
            dimension_semantics=("arbitrary",), vmem_limit_bytes=VMEM_LIMIT),
        name="unpermute",
    )(slot_of_tok, y_slots.reshape(-1, SUBLANES, LANES))


def _moe_block(h_rows, info, wg_b, wu_b, wd_b, ln2g, ln2b, alpha, ts):
    t = info.shape[0]
    disp = _dispatch(info, MOE_TILE)
    y_slots = _moe(h_rows.reshape(t, SUBLANES, LANES), disp, wg_b, wu_b, wd_b, ln2g, ln2b,
                   alpha, MOE_TILE)
    return _unpermute(y_slots, disp["slot_of_tok"], ts)


def kernel(x_prompt, x_sample, cache_k, cache_v, state_conv, page_table, w_in, lambda_q1, lambda_k1,
           lambda_q2, lambda_k2, subln_g, conv_w, conv_b, conv_ln_g, conv_ln_b, w_out, ln1_g, ln1_b,
           w_r1, b_r1, w_r2, b_r2, w_gate, w_up, w_down, ln2_g, ln2_b):
    depth = w_in.shape[0]
    n_p, s_p, _ = x_prompt.shape
    n_s, s_s, _ = x_sample.shape
    assert s_s == 1
    alpha = (2.0 * depth) ** 0.25
    pos_p = jnp.arange(s_p)
    pos_s = jnp.full((n_s,), PAST_LEN, jnp.int32)
    hp, hs = x_prompt, x_sample
    kp_l, vp_l, cp_l, ks_l, vs_l, cs_l = [], [], [], [], [], []
    for l in range(depth):
        lam_init = _lambda_init(l)
        lam = (jnp.exp(jnp.sum(lambda_q1[l].astype(F32) * lambda_k1[l]))
               - jnp.exp(jnp.sum(lambda_q2[l].astype(F32) * lambda_k2[l])) + lam_init)
        lam = lam.reshape(1).astype(F32)
        w_in_b = w_in[l].astype(BF16)
        w_out_b = w_out[l].astype(BF16)
        w_rt = jnp.concatenate(
            [w_r1[l], w_r2[l].transpose(1, 0, 2).reshape(D_MODEL, N_EXPERTS),
             jnp.zeros((D_MODEL, LANES - ROUTE_COLS), F32)], axis=1).astype(BF16)
        b_rt = jnp.concatenate(
            [b_r1[l], b_r2[l].reshape(-1), jnp.zeros((LANES - ROUTE_COLS,), F32)]).reshape(1, LANES)
        wg_b = w_gate[l].astype(BF16).reshape(N_GROUPS, EXPERTS_PER_GROUP, D_MODEL, D_EXPERT)
        wu_b = w_up[l].astype(BF16).reshape(N_GROUPS, EXPERTS_PER_GROUP, D_MODEL, D_EXPERT)
        wd_b = w_down[l].astype(BF16).reshape(N_GROUPS, EXPERTS_PER_GROUP, D_EXPERT, D_MODEL)
        sg = subln_g[l].reshape(1, V_DIM)
        cw, cb = conv_w[l], conv_b[l].reshape(1, CONV_CH)
        cg, cbeta = conv_ln_g[l].reshape(1, CONV_CH), conv_ln_b[l].reshape(1, CONV_CH)
        ln1g, ln1b = ln1_g[l].reshape(1, D_MODEL), ln1_b[l].reshape(1, D_MODEL)
        ln2g, ln2b = ln2_g[l].reshape(1, D_MODEL), ln2_b[l].reshape(1, D_MODEL)

        qb, k, v, u, kb, vb = _in_proj(hp.reshape(n_p * s_p, D_MODEL), w_in_b, pos_p, 512)
        att = _attn_prompt(qb.reshape(n_p, s_p, QK_WIDTH), kb.reshape(n_p, s_p, QK_WIDTH),
                           vb.reshape(n_p, s_p, ATT_WIDTH), lam, sg, lam_init, 512)
        u3 = u.reshape(n_p, s_p, CONV_CH)
        h_rows, info = _post_prompt(att, u3, hp, cw, cb, cg, cbeta, w_out_b, ln1g, ln1b,
                                    w_rt, b_rt, alpha, 512)
        kp_l.append(k.reshape(n_p, s_p, 2 * ATT_HEADS, HEAD_DIM))
        vp_l.append(v.reshape(n_p, s_p, ATT_HEADS, V_DIM))
        cp_l.append(u3[:, s_p - (CONV_K - 1):])
        hp = _moe_block(h_rows, info, wg_b, wu_b, wd_b, ln2g, ln2b, alpha, 512).reshape(n_p, s_p, D_MODEL)

        qb, k, v, u, kb, vb = _in_proj(hs.reshape(n_s, D_MODEL), w_in_b, pos_s, n_s)
        att = _attn_decode(page_table, qb, kb, vb, cache_k[l], cache_v[l], lam, sg, lam_init, 8)
        state_t = state_conv[l].transpose(1, 0, 2)
        h_rows, info = _post_sample(att.reshape(n_s, ATT_WIDTH), state_t, u, hs.reshape(n_s, D_MODEL),
                                    cw, cb, cg, cbeta, w_out_b, ln1g, ln1b, w_rt, b_rt, alpha)
        ks_l.append(k.reshape(n_s, 1, 2 * ATT_HEADS, HEAD_DIM))
        vs_l.append(v.reshape(n_s, 1, ATT_HEADS, V_DIM))
        cs_l.append(jnp.concatenate([state_conv[l][:, 1:], u[:, None, :]], axis=1))
        hs = _moe_block(h_rows, info, wg_b, wu_b, wd_b, ln2g, ln2b, alpha, n_s).reshape(n_s, 1, D_MODEL)
    return (hp, hs, jnp.stack(kp_l), jnp.stack(vp_l), jnp.stack(cp_l),
            jnp.stack(ks_l), jnp.stack(vs_l), jnp.stack(cs_l))
```

```python
import functools
import math

import jax
import jax.numpy as jnp
from jax import lax
from jax.experimental import pallas as pl
from jax.experimental.pallas import tpu as pltpu

D_MODEL = 1024
PAST_LEN = 16384
PAGE_SIZE = 128
ATT_HEADS = 4
HEAD_DIM = 64
V_DIM = 2 * HEAD_DIM
ATT_WIDTH = ATT_HEADS * V_DIM
QK_WIDTH = 2 * ATT_HEADS * HEAD_DIM
CONV_CH = D_MODEL - ATT_WIDTH
CONV_K = 31
IN_WIDTH = 2 * QK_WIDTH + ATT_WIDTH + 2 * CONV_CH
N_GROUPS = 4
EXPERTS_PER_GROUP = 8
N_EXPERTS = N_GROUPS * EXPERTS_PER_GROUP
D_EXPERT = 256
ROPE_THETA = 10000.0
LN_EPS = 1e-5

LANES = 128
SUBLANES = 8
HALO = 32
PAIRS_PER_GROUP = EXPERTS_PER_GROUP * (EXPERTS_PER_GROUP - 1) // 2
N_BUCKETS = N_GROUPS * PAIRS_PER_GROUP
ROUTE_COLS = N_GROUPS + N_EXPERTS
MOE_TILE = 128
DECODE_ROWS = 16
Q_SCALE = HEAD_DIM ** -0.5 * math.log2(math.e)
VMEM_LIMIT = 56 * 1024 * 1024

F32 = jnp.float32
BF16 = jnp.bfloat16


def _lambda_init(layer):
    return 0.8 - 0.6 * math.exp(-0.3 * layer)


def _layer_norm(x, g, b):
    mu = jnp.mean(x, axis=-1, keepdims=True)
    xc = x - mu
    var = jnp.mean(xc * xc, axis=-1, keepdims=True)
    return xc * lax.rsqrt(var + LN_EPS) * g + b


def _sigmoid(x):
    return 1.0 / (1.0 + jnp.exp(-x))


def _rope_tables(pos):
    half = HEAD_DIM // 2
    inv_freq = ROPE_THETA ** (-jnp.arange(half, dtype=F32) / half)
    ang = pos.astype(F32)[:, None] * inv_freq[None, :]
    cos, sin = jnp.cos(ang), jnp.sin(ang)
    zero = jnp.zeros_like(sin)
    cos_t = jnp.tile(cos, (1, LANES // half))
    sin_lo = jnp.tile(jnp.concatenate([-sin, zero], axis=1), (1, LANES // HEAD_DIM))
    sin_hi = jnp.tile(jnp.concatenate([zero, sin], axis=1), (1, LANES // HEAD_DIM))
    return cos_t, sin_lo, sin_hi


def _in_proj_body(x_ref, w_ref, cos_ref, slo_ref, shi_ref,
                  q_ref, k_ref, v_ref, u_ref, kb_ref, vb_ref):
    xb = x_ref[...].astype(BF16)
    cos, slo, shi = cos_ref[...], slo_ref[...], shi_ref[...]

    def seg(c0, width):
        return jnp.dot(xb, w_ref[:, c0:c0 + width], preferred_element_type=F32)

    def rope(h, j):
        xj = h[:, j * LANES:(j + 1) * LANES]
        return (xj * cos + pltpu.roll(xj, LANES - HEAD_DIM // 2, 1) * slo
                + pltpu.roll(xj, HEAD_DIM // 2, 1) * shi)

    hq = seg(0, QK_WIDTH)
    for j in range(QK_WIDTH // LANES):
        sl = slice(j * LANES, (j + 1) * LANES)
        q_ref[:, sl] = (rope(hq, j) * Q_SCALE).astype(BF16)
    hk = seg(QK_WIDTH, QK_WIDTH)
    for j in range(QK_WIDTH // LANES):
        sl = slice(j * LANES, (j + 1) * LANES)
        kr = rope(hk, j)
        k_ref[:, sl] = kr
        kb_ref[:, sl] = kr.astype(BF16)
    hv = seg(2 * QK_WIDTH, ATT_WIDTH)
    v_ref[...] = hv
    vb_ref[...] = hv.astype(BF16)
    ha = seg(2 * QK_WIDTH + ATT_WIDTH, CONV_CH)
    hg = seg(2 * QK_WIDTH + ATT_WIDTH + CONV_CH, CONV_CH)
    u_ref[...] = ha * _sigmoid(hg)


def _in_proj(x2d, w_bf, pos, tm):
    t = x2d.shape[0]
    cos_t, slo_t, shi_t = _rope_tables(pos)
    tab_blocks = cos_t.shape[0] // tm
    row = lambda i: (i, 0)
    tab = lambda i: (i % tab_blocks, 0)
    outs = pl.pallas_call(
        _in_proj_body,
        grid=(t // tm,),
        in_specs=[
            pl.BlockSpec((tm, D_MODEL), row),
            pl.BlockSpec((D_MODEL, IN_WIDTH), lambda i: (0, 0)),
            pl.BlockSpec((tm, LANES), tab),
            pl.BlockSpec((tm, LANES), tab),
            pl.BlockSpec((tm, LANES), tab),
        ],
        out_specs=[
            pl.BlockSpec((tm, QK_WIDTH), row),
            pl.BlockSpec((tm, QK_WIDTH), row),
            pl.BlockSpec((tm, ATT_WIDTH), row),
            pl.BlockSpec((tm, CONV_CH), row),
            pl.BlockSpec((tm, QK_WIDTH), row),
            pl.BlockSpec((tm, ATT_WIDTH), row),
        ],
        out_shape=[
            jax.ShapeDtypeStruct((t, QK_WIDTH), BF16),
            jax.ShapeDtypeStruct((t, QK_WIDTH), F32),
            jax.ShapeDtypeStruct((t, ATT_WIDTH), F32),
            jax.ShapeDtypeStruct((t, CONV_CH), F32),
            jax.ShapeDtypeStruct((t, QK_WIDTH), BF16),
            jax.ShapeDtypeStruct((t, ATT_WIDTH), BF16),
        ],
        compiler_params=pltpu.CompilerParams(
            dimension_semantics=("parallel",), vmem_limit_bytes=VMEM_LIMIT),
        name="in_proj",
    )(x2d, w_bf, cos_t, slo_t, shi_t)
    return outs


def _sub_ln(o, g, lam_init):
    ms = jnp.mean(o * o, axis=-1, keepdims=True)
    return o * lax.rsqrt(ms + LN_EPS) * g * (1.0 - lam_init)


def _attn_body(lam_ref, q_ref, k_ref, v_ref, g_ref, o_ref, qt_sc, vt_sc, s_sc, mc_sc, p_sc, al_sc,
               m_sc, l_sc, acc_sc, *, bq, cw, lam_init):
    i = pl.program_id(2)
    n_kv = v_ref.shape[1] // bq

    @pl.when(i == 0)
    def _():
        for c in range(n_kv):
            vt_sc[c] = v_ref[0, c * bq:(c + 1) * bq, :].astype(F32).T.astype(BF16)

    qt = q_ref[0].astype(F32).T
    row = lax.broadcasted_iota(jnp.int32, qt.shape, 0)
    qt_sc[:, :bq] = jnp.where(row < HEAD_DIM, qt, 0.0).astype(BF16)
    qt_sc[:, bq:] = jnp.where(row >= HEAD_DIM, qt, 0.0).astype(BF16)
    m_sc[...] = jnp.full(m_sc.shape, -jnp.inf, F32)
    l_sc[...] = jnp.zeros(l_sc.shape, F32)
    acc_sc[...] = jnp.zeros(acc_sc.shape, F32)

    chunks = [slice(c0, c0 + cw) for c0 in range(0, 2 * bq, cw)]

    def scores(j, diagonal=False):
        slot = j % 2
        k = k_ref[0, pl.ds(pl.multiple_of(j * bq, bq), bq), :]
        for cols in chunks:
            s = jnp.dot(k, qt_sc[:, cols], preferred_element_type=F32)
            if diagonal:
                key = lax.broadcasted_iota(jnp.int32, s.shape, 0)
                qry = lax.broadcasted_iota(jnp.int32, s.shape, 1) + (cols.start % bq)
                s = jnp.where(key <= qry, s, -jnp.inf)
            s_sc[slot, :, cols] = s
            mc_sc[slot, :, cols] = jnp.max(s, axis=0, keepdims=True)

    def softmax(j):
        slot = j % 2
        for cols in chunks:
            m_prev = m_sc[:, cols]
            m_new = jnp.maximum(m_prev, mc_sc[slot, :, cols])
            alpha = jnp.exp2(m_prev - m_new)
            p = jnp.exp2(s_sc[slot, :, cols] - m_new)
            l_sc[:, cols] = alpha * l_sc[:, cols] + jnp.sum(p, axis=0, keepdims=True)
            p_sc[slot, :, cols] = p.astype(BF16)
            al_sc[slot, :, cols] = alpha
            m_sc[:, cols] = m_new

    def weigh(j):
        slot = j % 2
        vt = vt_sc[j]
        for cols in chunks:
            acc_sc[:, cols] = al_sc[slot, :, cols] * acc_sc[:, cols] + jnp.dot(
                vt, p_sc[slot, :, cols], preferred_element_type=F32)

    @pl.when(i == 0)
    def _():
        scores(0, True)
        softmax(0)
        weigh(0)

    @pl.when(i == 1)
    def _():
        scores(0)
        scores(1, True)
        softmax(0)
        softmax(1)
        weigh(0)
        weigh(1)

    @pl.when(i >= 2)
    def _():
        scores(0)
        scores(1)
        softmax(0)

        def body(j, carry):
            scores(j + 2)
            weigh(j)
            softmax(j + 1)
            return carry

        lax.fori_loop(0, i - 2, body, 0)
        scores(i, True)
        weigh(i - 2)
        softmax(i - 1)
        softmax(i)
        weigh(i - 1)
        weigh(i)

    lam = lam_ref[0]
    o = acc_sc[:, :bq] / l_sc[:, :bq] - lam * (acc_sc[:, bq:] / l_sc[:, bq:])
    ms = jnp.mean(o * o, axis=0, keepdims=True)
    o = o * lax.rsqrt(ms + LN_EPS) * g_ref[...] * (1.0 - lam_init)
    o_ref[0] = o.T.astype(BF16)


def _attn_prompt(qb, kb, vb, lam, subln_g, lam_init, bq, cw):
    n, s, _ = qb.shape
    body = functools.partial(_attn_body, bq=bq, cw=cw, lam_init=lam_init)
    return pl.pallas_call(
        body,
        grid=(n, ATT_HEADS, s // bq),
        in_specs=[
            pl.BlockSpec(memory_space=pltpu.SMEM),
            pl.BlockSpec((1, bq, V_DIM), lambda b, h, i: (b, i, h)),
            pl.BlockSpec((1, s, V_DIM), lambda b, h, i: (b, 0, h)),
            pl.BlockSpec((1, s, V_DIM), lambda b, h, i: (b, 0, h)),
            pl.BlockSpec((V_DIM, 1), lambda b, h, i: (0, 0)),
        ],
        out_specs=pl.BlockSpec((1, bq, V_DIM), lambda b, h, i: (b, i, h)),
        scratch_shapes=[
            pltpu.VMEM((V_DIM, 2 * bq), BF16),
            pltpu.VMEM((s // bq, V_DIM, bq), BF16),
            pltpu.VMEM((2, bq, 2 * bq), F32),
            pltpu.VMEM((2, 1, 2 * bq), F32),
            pltpu.VMEM((2, bq, 2 * bq), BF16),
            pltpu.VMEM((2, 1, 2 * bq), F32),
            pltpu.VMEM((1, 2 * bq), F32),
            pltpu.VMEM((1, 2 * bq), F32),
            pltpu.VMEM((V_DIM, 2 * bq), F32),
        ],
        out_shape=jax.ShapeDtypeStruct((n, s, ATT_WIDTH), BF16),
        compiler_params=pltpu.CompilerParams(
            dimension_semantics=("parallel", "parallel", "arbitrary"), vmem_limit_bytes=VMEM_LIMIT),
        name="attn_prompt",
    )(lam, qb, kb, vb, subln_g.reshape(V_DIM, 1))


def _decode_body(pt_ref, lam_ref, q_ref, kn_ref, vn_ref, g_ref, *rest, pages, lam_init):
    k_refs = rest[:pages]
    v_refs = rest[pages:2 * pages]
    o_ref, qbd_sc, m_sc, l_sc, acc_sc = rest[2 * pages:]
    j = pl.program_id(1)

    @pl.when(j == 0)
    def _():
        q = jnp.broadcast_to(q_ref[0].astype(F32), (DECODE_ROWS, QK_WIDTH))
        row = lax.broadcasted_iota(jnp.int32, q.shape, 0)
        lane = lax.broadcasted_iota(jnp.int32, q.shape, 1)
        qbd_sc[...] = jnp.where((lane >= row * HEAD_DIM) & (lane < (row + 1) * HEAD_DIM), q, 0.0)
        m_sc[...] = jnp.full(m_sc.shape, -jnp.inf, F32)
        l_sc[...] = jnp.zeros(l_sc.shape, F32)
        acc_sc[...] = jnp.zeros(acc_sc.shape, F32)

    qbd_f = qbd_sc[...]
    qbd = qbd_f.astype(BF16)
    s = jnp.concatenate(
        [jnp.dot(qbd, kr[0].astype(BF16), preferred_element_type=F32) for kr in k_refs], axis=1)
    m_prev = m_sc[...]
    m_new = jnp.maximum(m_prev, jnp.max(s, axis=1, keepdims=True))
    alpha = jnp.exp2(m_prev - m_new)
    p = jnp.exp2(s - m_new[:, :1])
    l_sc[...] = alpha * l_sc[...] + jnp.sum(p, axis=1, keepdims=True)
    pb = p.astype(BF16)
    pv = []
    for h in range(ATT_HEADS):
        acc_h = jnp.zeros((DECODE_ROWS, V_DIM), F32)
        for idx, vr in enumerate(v_refs):
            v_h = vr[0, pl.ds(h, PAGE_SIZE, stride=ATT_HEADS), :].astype(BF16)
            acc_h = acc_h + jnp.dot(pb[:, idx * PAGE_SIZE:(idx + 1) * PAGE_SIZE], v_h,
                                    preferred_element_type=F32)
        pv.append(acc_h)
    acc_sc[...] = alpha[:, :1] * acc_sc[...] + jnp.concatenate(pv, axis=1)
    m_sc[...] = m_new

    @pl.when(j == pl.num_programs(1) - 1)
    def _():
        kn = kn_ref[0].astype(F32)
        vn = vn_ref[0].astype(F32)
        s_self = jnp.sum(qbd_f * kn, axis=1, keepdims=True)
        m_last = m_sc[...]
        m_fin = jnp.maximum(m_last, s_self)
        a_fin = jnp.exp2(m_last - m_fin)
        p_self = jnp.exp2(s_self - m_fin[:, :1])
        l_fin = l_sc[...] * a_fin + p_self
        acc = acc_sc[...] * a_fin[:, :1] + p_self.astype(BF16).astype(F32) * vn
        o = acc / l_fin[:, :1]
        row = lax.broadcasted_iota(jnp.int32, o.shape, 0)
        head = lax.broadcasted_iota(jnp.int32, o.shape, 1) // V_DIM
        lam = lam_ref[0]
        coef = jnp.where(row == 2 * head, 1.0, jnp.where(row == 2 * head + 1, -lam, 0.0))
        oc = jnp.sum(o * coef, axis=0, keepdims=True)
        g = g_ref[...]
        for h in range(ATT_HEADS):
            sl = slice(h * V_DIM, (h + 1) * V_DIM)
            o_ref[0, :, sl] = _sub_ln(oc[:, sl], g, lam_init).astype(BF16)


def _attn_decode(page_table, qb, kb_new, vb_new, cache_k, cache_v, layer, lam, subln_g, lam_init, pages):
    n, n_pages = page_table.shape
    depth, n_pool = cache_k.shape[:2]
    ck = cache_k.transpose(0, 1, 3, 4, 2).reshape(depth * n_pool, QK_WIDTH, PAGE_SIZE)
    cv = cache_v.reshape(depth * n_pool, PAGE_SIZE * ATT_HEADS, V_DIM)
    pt_flat = page_table.reshape(-1) + layer * n_pool

    def page_map(idx):
        return lambda b, j, pt: (pt[b * n_pages + j * pages + idx], 0, 0)

    row = lambda b, j, pt: (b, 0, 0)
    body = functools.partial(_decode_body, pages=pages, lam_init=lam_init)
    n_comp = DECODE_ROWS
    assert V_DIM == PAGE_SIZE == LANES
    return pl.pallas_call(
        body,
        grid_spec=pltpu.PrefetchScalarGridSpec(
            num_scalar_prefetch=1,
            grid=(n, n_pages // pages),
            in_specs=(
                [pl.BlockSpec(memory_space=pltpu.SMEM),
                 pl.BlockSpec((1, 1, QK_WIDTH), row),
                 pl.BlockSpec((1, 1, QK_WIDTH), row),
                 pl.BlockSpec((1, 1, ATT_WIDTH), row),
                 pl.BlockSpec((1, V_DIM), lambda b, j, pt: (0, 0))]
                + [pl.BlockSpec((1, QK_WIDTH, PAGE_SIZE), page_map(i)) for i in range(pages)]
                + [pl.BlockSpec((1, PAGE_SIZE * ATT_HEADS, V_DIM), page_map(i)) for i in range(pages)]),
            out_specs=pl.BlockSpec((1, 1, ATT_WIDTH), row),
            scratch_shapes=[
                pltpu.VMEM((n_comp, QK_WIDTH), F32),
                pltpu.VMEM((n_comp, LANES), F32),
                pltpu.VMEM((n_comp, LANES), F32),
                pltpu.VMEM((n_comp, ATT_WIDTH), F32),
            ],
        ),
        out_shape=jax.ShapeDtypeStruct((n, 1, ATT_WIDTH), BF16),
        compiler_params=pltpu.CompilerParams(
            dimension_semantics=("parallel", "arbitrary"), vmem_limit_bytes=VMEM_LIMIT),
        name="attn_decode",
    )(pt_flat, lam, qb.reshape(n, 1, QK_WIDTH), kb_new.reshape(n, 1, QK_WIDTH),
      vb_new.reshape(n, 1, ATT_WIDTH), subln_g, *([ck] * pages), *([cv] * pages))


def _route(logits):
    lane = lax.broadcasted_iota(jnp.int32, logits.shape, 1)
    lane_f = lane.astype(F32)
    neg = -jnp.inf
    big = float(LANES)

    def first_argmax(mask, vmax):
        return jnp.min(jnp.where(mask & (logits == vmax), lane_f, big), axis=1, keepdims=True)

    g_mask = lane < N_GROUPS
    g_max = jnp.max(jnp.where(g_mask, logits, neg), axis=1, keepdims=True)
    g_sel = first_argmax(g_mask, g_max)
    p_g = 1.0 / jnp.sum(jnp.where(g_mask, jnp.exp(logits - g_max), 0.0), axis=1, keepdims=True)
    lo = N_GROUPS + EXPERTS_PER_GROUP * g_sel
    e_mask = (lane_f >= lo) & (lane_f < lo + EXPERTS_PER_GROUP)
    v1 = jnp.max(jnp.where(e_mask, logits, neg), axis=1, keepdims=True)
    i1 = first_argmax(e_mask, v1)
    e_mask2 = e_mask & (lane_f != i1)
    v2 = jnp.max(jnp.where(e_mask2, logits, neg), axis=1, keepdims=True)
    i2 = first_argmax(e_mask2, v2)
    t = jnp.exp(v2 - v1)
    den = 1.0 + t
    w_first = (1.0 / den) * p_g
    w_second = (t / den) * p_g
    a = i1 - lo
    b = i2 - lo
    first_low = a < b
    e_lo = jnp.minimum(a, b)
    e_hi = jnp.maximum(a, b)
    pair = e_lo * (EXPERTS_PER_GROUP - 1) - e_lo * (e_lo - 1.0) * 0.5 + (e_hi - e_lo - 1.0)
    bucket = g_sel * PAIRS_PER_GROUP + pair
    w_lo = jnp.where(first_low, w_first, w_second)
    w_hi = jnp.where(first_low, w_second, w_first)
    return jnp.where(lane == 0, bucket, jnp.where(lane == 1, w_lo, jnp.where(lane == 2, w_hi, 0.0)))


def _mix_ln_route(att_b, c, x, wout_ref, ln1g_ref, ln1b_ref, wrt_ref, brt_ref, alpha):
    cb = (c * _sigmoid(c)).astype(BF16)
    mix = (jnp.dot(att_b, wout_ref[:ATT_WIDTH, :], preferred_element_type=F32)
           + jnp.dot(cb, wout_ref[ATT_WIDTH:, :], preferred_element_type=F32))
    h = _layer_norm(alpha * x + mix, ln1g_ref[...], ln1b_ref[...])
    logits = jnp.dot(h.astype(BF16), wrt_ref[...], preferred_element_type=F32) + brt_ref[...]
    return h, _route(logits)


def _store_token_major(dst_ref, val):
    rows = val.shape[0]
    for c in range(D_MODEL // LANES):
        dst_ref[pl.ds(c, rows, stride=SUBLANES), :] = val[:, c * LANES:(c + 1) * LANES]


def _load_token_major(src_ref, rows):
    return jnp.concatenate([src_ref[pl.ds(c, rows, stride=SUBLANES), :]
                            for c in range(D_MODEL // LANES)], axis=1)


def _post_prompt_body(att_ref, u_ref, halo_ref, x_ref, cw_ref, cb_ref, cg_ref, cbeta_ref,
                      wout_ref, ln1g_ref, ln1b_ref, wrt_ref, brt_ref,
                      h_ref, info_ref, ext_sc, c_sc, *, ts, alpha, row_chunk):
    t = pl.program_id(1)

    @pl.when(t == 0)
    def _():
        ext_sc[0:HALO, :] = jnp.zeros((HALO, CONV_CH), F32)

    @pl.when(t > 0)
    def _():
        ext_sc[0:HALO, :] = halo_ref[0]

    ext_sc[HALO:HALO + ts, :] = u_ref[0]
    first = HALO - (CONV_K - 1)
    bias = cb_ref[...]
    for r0 in range(0, ts, row_chunk):
        acc = jnp.broadcast_to(bias, (row_chunk, CONV_CH))
        for kk in range(CONV_K):
            acc = acc + ext_sc[r0 + first + kk:r0 + first + kk + row_chunk, :] * cw_ref[kk:kk + 1, :]
        c_sc[r0:r0 + row_chunk, :] = _layer_norm(acc, cg_ref[...], cbeta_ref[...])
    h, info = _mix_ln_route(att_ref[0], c_sc[...], x_ref[0], wout_ref, ln1g_ref, ln1b_ref,
                            wrt_ref, brt_ref, alpha)
    _store_token_major(h_ref, h)
    info_ref[...] = info


def _post_prompt(att, u, x, cw, cb, cg, cbeta, wout_b, ln1g, ln1b, wrt_b, brt, alpha, ts):
    n, s, _ = x.shape
    t_blocks = s // ts
    body = functools.partial(_post_prompt_body, ts=ts, alpha=alpha, row_chunk=64)
    const = lambda b, t: (0, 0)
    return pl.pallas_call(
        body,
        grid=(n, t_blocks),
        in_specs=[
            pl.BlockSpec((1, ts, ATT_WIDTH), lambda b, t: (b, t, 0)),
            pl.BlockSpec((1, ts, CONV_CH), lambda b, t: (b, t, 0)),
            pl.BlockSpec((1, HALO, CONV_CH), lambda b, t: (b, jnp.maximum(t * (ts // HALO) - 1, 0), 0)),
            pl.BlockSpec((1, ts, D_MODEL), lambda b, t: (b, t, 0)),
            pl.BlockSpec((CONV_K, CONV_CH), const),
            pl.BlockSpec((1, CONV_CH), const),
            pl.BlockSpec((1, CONV_CH), const),
            pl.BlockSpec((1, CONV_CH), const),
            pl.BlockSpec((D_MODEL, D_MODEL), const),
            pl.BlockSpec((1, D_MODEL), const),
            pl.BlockSpec((1, D_MODEL), const),
            pl.BlockSpec((D_MODEL, LANES), const),
            pl.BlockSpec((1, LANES), const),
        ],
        out_specs=[
            pl.BlockSpec((ts * SUBLANES, LANES), lambda b, t: (b * t_blocks + t, 0)),
            pl.BlockSpec((ts, LANES), lambda b, t: (b * t_blocks + t, 0)),
        ],
        out_shape=[
            jax.ShapeDtypeStruct((n * s * SUBLANES, LANES), F32),
            jax.ShapeDtypeStruct((n * s, LANES), F32),
        ],
        scratch_shapes=[
            pltpu.VMEM((HALO + ts, CONV_CH), F32),
            pltpu.VMEM((ts, CONV_CH), F32),
        ],
        compiler_params=pltpu.CompilerParams(
            dimension_semantics=("parallel", "arbitrary"), vmem_limit_bytes=VMEM_LIMIT),
        name="post_prompt",
    )(att, u, u, x, cw, cb, cg, cbeta, wout_b, ln1g, ln1b, wrt_b, brt)


def _post_sample_body(att_ref, st_ref, u_ref, x_ref, cw_ref, cb_ref, cg_ref, cbeta_ref,
                      wout_ref, ln1g_ref, ln1b_ref, wrt_ref, brt_ref, h_ref, info_ref, *, alpha):
    acc = cb_ref[...] + u_ref[...] * cw_ref[CONV_K - 1:CONV_K, :]
    for kk in range(CONV_K - 1):
        acc = acc + st_ref[kk] * cw_ref[kk:kk + 1, :]
    c = _layer_norm(acc, cg_ref[...], cbeta_ref[...])
    h, info = _mix_ln_route(att_ref[...], c, x_ref[...], wout_ref, ln1g_ref, ln1b_ref,
                            wrt_ref, brt_ref, alpha)
    _store_token_major(h_ref, h)
    info_ref[...] = info


def _post_sample(att, state_t, u, x, cw, cb, cg, cbeta, wout_b, ln1g, ln1b, wrt_b, brt, alpha):
    n = x.shape[0]
    body = functools.partial(_post_sample_body, alpha=alpha)
    return pl.pallas_call(
        body,
        out_shape=[
            jax.ShapeDtypeStruct((n * SUBLANES, LANES), F32),
            jax.ShapeDtypeStruct((n, LANES), F32),
        ],
        compiler_params=pltpu.CompilerParams(vmem_limit_bytes=VMEM_LIMIT),
        name="post_sample",
    )(att, state_t, u, x, cw, cb, cg, cbeta, wout_b, ln1g, ln1b, wrt_b, brt)


def _dispatch(info, tm):
    t = info.shape[0]
    n_tiles = t // tm + min(N_BUCKETS, t) + 1
    bucket = info[:, 0].astype(jnp.int32)
    order = jnp.argsort(bucket, stable=True).astype(jnp.int32)
    rank_of_tok = jnp.argsort(order).astype(jnp.int32)
    sorted_bucket = bucket[order]
    bucket_ids = jnp.arange(N_BUCKETS, dtype=jnp.int32)
    row_start = jnp.searchsorted(sorted_bucket, bucket_ids, side="left").astype(jnp.int32)
    row_end = jnp.searchsorted(sorted_bucket, bucket_ids, side="right").astype(jnp.int32)
    counts = row_end - row_start
    tiles_per = (counts + tm - 1) // tm
    tile_end = jnp.cumsum(tiles_per)
    tile_start = tile_end - tiles_per
    slot_of_tok = (tile_start[bucket] * tm + rank_of_tok - row_start[bucket]).astype(jnp.int32)
    tile_ids = jnp.arange(n_tiles, dtype=jnp.int32)
    tile_bucket = jnp.searchsorted(tile_end, tile_ids, side="right").astype(jnp.int32)
    used = tile_bucket < N_BUCKETS
    tb = jnp.minimum(tile_bucket, N_BUCKETS - 1)
    tile_in_bucket = tile_ids - tile_start[tb]
    n_valid = jnp.where(used, jnp.clip(counts[tb] - tile_in_bucket * tm, 0, tm), 0)
    first_row = jnp.where(used, row_start[tb] + tile_in_bucket * tm, 0)
    last_bucket = tb[jnp.maximum(tile_end[-1] - 1, 0)]
    tb = jnp.where(used, tb, last_bucket)
    pair = tb % PAIRS_PER_GROUP
    lo_tab, hi_tab = [], []
    for lo in range(EXPERTS_PER_GROUP):
        for hi in range(lo + 1, EXPERTS_PER_GROUP):
            lo_tab.append(lo)
            hi_tab.append(hi)
    e_lo = jnp.asarray(lo_tab, jnp.int32)[pair]
    e_hi = jnp.asarray(hi_tab, jnp.int32)[pair]
    return dict(order=order, slot_of_tok=slot_of_tok, first_row=first_row.astype(jnp.int32),
                tile_group=(tb // PAIRS_PER_GROUP).astype(jnp.int32), e_lo=e_lo, e_hi=e_hi,
                n_valid=n_valid.astype(jnp.int32), n_tiles=n_tiles)


def _gate_weights(logits, group, e_lo, e_hi):
    lane = lax.broadcasted_iota(jnp.int32, logits.shape, 1)

    def column(idx):
        return jnp.sum(jnp.where(lane == idx, logits, 0.0), axis=1, keepdims=True)

    g_mask = lane < N_GROUPS
    g_max = jnp.max(jnp.where(g_mask, logits, -jnp.inf), axis=1, keepdims=True)
    g_sum = jnp.sum(jnp.where(g_mask, jnp.exp(logits - g_max), 0.0), axis=1, keepdims=True)
    p_g = jnp.exp(column(group) - g_max) / g_sum
    first = N_GROUPS + EXPERTS_PER_GROUP * group
    v_lo = column(first + e_lo)
    v_hi = column(first + e_hi)
    v_max = jnp.maximum(v_lo, v_hi)
    t_lo = jnp.exp(v_lo - v_max)
    t_hi = jnp.exp(v_hi - v_max)
    den = t_lo + t_hi
    return (t_lo / den) * p_g, (t_hi / den) * p_g


def _moe_body(grp_ref, elo_ref, ehi_ref, nv_ref, first_ref, order_ref,
              h_hbm, wrt_ref, brt_ref, wg_ref, wu_ref, wd_ref, g2_ref, b2_ref,
              y_ref, xbuf, sem, *, tm, alpha):
    t = pl.program_id(0)
    nv = nv_ref[t]

    @pl.when(nv == 0)
    def _():
        y_ref[...] = jnp.zeros(y_ref.shape, F32)

    @pl.when(nv > 0)
    def _():
        base = first_ref[t]

        def row_copy(r, tok):
            return pltpu.make_async_copy(
                h_hbm.at[tok], xbuf.at[pl.ds(pl.multiple_of(r * SUBLANES, SUBLANES), SUBLANES)], sem)

        def issue(r, carry):
            row_copy(r, order_ref[base + jnp.minimum(r, nv - 1)]).start()
            return carry

        def wait(r, carry):
            row_copy(r, 0).wait()
            return carry

        lax.fori_loop(0, tm, issue, 0)
        lax.fori_loop(0, tm, wait, 0)
        x = _load_token_major(xbuf, tm)
        xb = x.astype(BF16)

        def expert(e):
            gate = jnp.dot(xb, wg_ref[0, e], preferred_element_type=F32)
            up = jnp.dot(xb, wu_ref[0, e], preferred_element_type=F32)
            hidden = (gate * _sigmoid(gate) * up).astype(BF16)
            return jnp.dot(hidden, wd_ref[0, e], preferred_element_type=F32)

        logits = jnp.dot(xb, wrt_ref[...], preferred_element_type=F32) + brt_ref[...]
        w_lo, w_hi = _gate_weights(logits, grp_ref[t], elo_ref[t], ehi_ref[t])
        moe = w_lo * expert(elo_ref[t]) + w_hi * expert(ehi_ref[t])
        y = _layer_norm(alpha * x + moe, g2_ref[...], b2_ref[...])
        _store_token_major(y_ref, y)


def _moe(h_tok, disp, wrt_b, brt, wg_b, wu_b, wd_b, ln2g, ln2b, alpha, tm):
    n_tiles = disp["n_tiles"]
    body = functools.partial(_moe_body, tm=tm, alpha=alpha)
    wmap = lambda t, grp, *_: (grp[t], 0, 0, 0)
    const = lambda t, *_: (0, 0)
    rows = lambda t, *_: (t, 0)
    return pl.pallas_call(
        body,
        grid_spec=pltpu.PrefetchScalarGridSpec(
            num_scalar_prefetch=6,
            grid=(n_tiles,),
            in_specs=[
                pl.BlockSpec(memory_space=pl.ANY),
                pl.BlockSpec((D_MODEL, LANES), const),
                pl.BlockSpec((1, LANES), const),
                pl.BlockSpec((1, EXPERTS_PER_GROUP, D_MODEL, D_EXPERT), wmap),
                pl.BlockSpec((1, EXPERTS_PER_GROUP, D_MODEL, D_EXPERT), wmap),
                pl.BlockSpec((1, EXPERTS_PER_GROUP, D_EXPERT, D_MODEL), wmap),
                pl.BlockSpec((1, D_MODEL), const),
                pl.BlockSpec((1, D_MODEL), const),
            ],
            out_specs=pl.BlockSpec((tm * SUBLANES, LANES), rows),
            scratch_shapes=[
                pltpu.VMEM((tm * SUBLANES, LANES), F32),
                pltpu.SemaphoreType.DMA(()),
            ],
        ),
        out_shape=jax.ShapeDtypeStruct((n_tiles * tm * SUBLANES, LANES), F32),
        compiler_params=pltpu.CompilerParams(
            dimension_semantics=("arbitrary",), vmem_limit_bytes=VMEM_LIMIT),
        name="moe",
    )(disp["tile_group"], disp["e_lo"], disp["e_hi"], disp["n_valid"], disp["first_row"], disp["order"],
      h_tok, wrt_b, brt, wg_b, wu_b, wd_b, ln2g, ln2b)


def _unpermute_body(slot_ref, y_hbm, o_ref, buf, sem, *, ts):
    base = pl.program_id(0) * ts

    def row_copy(r, slot):
        return pltpu.make_async_copy(
            y_hbm.at[slot], buf.at[pl.ds(pl.multiple_of(r * SUBLANES, SUBLANES), SUBLANES)], sem)

    def issue(r, carry):
        row_copy(r, slot_ref[base + r]).start()
        return carry

    def wait(r, carry):
        row_copy(r, 0).wait()
        return carry

    lax.fori_loop(0, ts, issue, 0)
    lax.fori_loop(0, ts, wait, 0)
    for c in range(D_MODEL // LANES):
        o_ref[:, c * LANES:(c + 1) * LANES] = buf[pl.ds(c, ts, stride=SUBLANES), :]


def _unpermute(y_slots, slot_of_tok, ts):
    t = slot_of_tok.shape[0]
    body = functools.partial(_unpermute_body, ts=ts)
    return pl.pallas_call(
        body,
        grid_spec=pltpu.PrefetchScalarGridSpec(
            num_scalar_prefetch=1,
            grid=(t // ts,),
            in_specs=[pl.BlockSpec(memory_space=pl.ANY)],
            out_specs=pl.BlockSpec((ts, D_MODEL), lambda i, slot: (i, 0)),
            scratch_shapes=[
                pltpu.VMEM((ts * SUBLANES, LANES), F32),
                pltpu.SemaphoreType.DMA(()),
            ],
        ),
        out_shape=jax.ShapeDtypeStruct((t, D_MODEL), F32),
        compiler_params=pltpu.CompilerParams(
            dimension_semantics=("arbitrary",), vmem_limit_bytes=VMEM_LIMIT),
        name="unpermute",
    )(slot_of_tok, y_slots.reshape(-1, SUBLANES, LANES))


def _moe_block(h_rows, info, wrt_b, brt, wg_b, wu_b, wd_b, ln2g, ln2b, alpha, ts):
    t = info.shape[0]
    disp = _dispatch(info, MOE_TILE)
    y_slots = _moe(h_rows.reshape(t, SUBLANES, LANES), disp, wrt_b, brt, wg_b, wu_b, wd_b, ln2g, ln2b,
                   alpha, MOE_TILE)
    return _unpermute(y_slots, disp["slot_of_tok"], ts)


def kernel(x_prompt, x_sample, cache_k, cache_v, state_conv, page_table, w_in, lambda_q1, lambda_k1,
           lambda_q2, lambda_k2, subln_g, conv_w, conv_b, conv_ln_g, conv_ln_b, w_out, ln1_g, ln1_b,
           w_r1, b_r1, w_r2, b_r2, w_gate, w_up, w_down, ln2_g, ln2_b):
    depth = w_in.shape[0]
    n_p, s_p, _ = x_prompt.shape
    n_s, s_s, _ = x_sample.shape
    assert s_s == 1
    alpha = (2.0 * depth) ** 0.25
    pos_p = jnp.arange(s_p)
    pos_s = jnp.full((n_s,), PAST_LEN, jnp.int32)
    hp, hs = x_prompt, x_sample
    kp_l, vp_l, cp_l, ks_l, vs_l, cs_l = [], [], [], [], [], []
    for l in range(depth):
        lam_init = _lambda_init(l)
        lam = (jnp.exp(jnp.sum(lambda_q1[l].astype(F32) * lambda_k1[l]))
               - jnp.exp(jnp.sum(lambda_q2[l].astype(F32) * lambda_k2[l])) + lam_init)
        lam = lam.reshape(1).astype(F32)
        w_in_b = w_in[l].astype(BF16)
        w_out_b = w_out[l].astype(BF16)
        w_rt = jnp.concatenate(
            [w_r1[l], w_r2[l].transpose(1, 0, 2).reshape(D_MODEL, N_EXPERTS),
             jnp.zeros((D_MODEL, LANES - ROUTE_COLS), F32)], axis=1).astype(BF16)
        b_rt = jnp.concatenate(
            [b_r1[l], b_r2[l].reshape(-1), jnp.zeros((LANES - ROUTE_COLS,), F32)]).reshape(1, LANES)
        wg_b = w_gate[l].astype(BF16).reshape(N_GROUPS, EXPERTS_PER_GROUP, D_MODEL, D_EXPERT)
        wu_b = w_up[l].astype(BF16).reshape(N_GROUPS, EXPERTS_PER_GROUP, D_MODEL, D_EXPERT)
        wd_b = w_down[l].astype(BF16).reshape(N_GROUPS, EXPERTS_PER_GROUP, D_EXPERT, D_MODEL)
        sg = subln_g[l].reshape(1, V_DIM)
        cw, cb = conv_w[l], conv_b[l].reshape(1, CONV_CH)
        cg, cbeta = conv_ln_g[l].reshape(1, CONV_CH), conv_ln_b[l].reshape(1, CONV_CH)
        ln1g, ln1b = ln1_g[l].reshape(1, D_MODEL), ln1_b[l].reshape(1, D_MODEL)
        ln2g, ln2b = ln2_g[l].reshape(1, D_MODEL), ln2_b[l].reshape(1, D_MODEL)

        qb, k, v, u, kb, vb = _in_proj(hp.reshape(n_p * s_p, D_MODEL), w_in_b, pos_p, 512)
        att = _attn_prompt(qb.reshape(n_p, s_p, QK_WIDTH), kb.reshape(n_p, s_p, QK_WIDTH),
                           vb.reshape(n_p, s_p, ATT_WIDTH), lam, sg, lam_init, 512, 256)
        u3 = u.reshape(n_p, s_p, CONV_CH)
        h_rows, info = _post_prompt(att, u3, hp, cw, cb, cg, cbeta, w_out_b, ln1g, ln1b,
                                    w_rt, b_rt, alpha, 512)
        kp_l.append(k.reshape(n_p, s_p, 2 * ATT_HEADS, HEAD_DIM))
        vp_l.append(v.reshape(n_p, s_p, ATT_HEADS, V_DIM))
        cp_l.append(u3[:, s_p - (CONV_K - 1):])
        hp = _moe_block(h_rows, info, w_rt, b_rt, wg_b, wu_b, wd_b, ln2g, ln2b, alpha,
                        512).reshape(n_p, s_p, D_MODEL)

        qb, k, v, u, kb, vb = _in_proj(hs.reshape(n_s, D_MODEL), w_in_b, pos_s, n_s)
        att = _attn_decode(page_table, qb, kb, vb, cache_k, cache_v, l, lam, sg, lam_init, 8)
        state_t = state_conv[l].transpose(1, 0, 2)
        h_rows, info = _post_sample(att.reshape(n_s, ATT_WIDTH), state_t, u, hs.reshape(n_s, D_MODEL),
                                    cw, cb, cg, cbeta, w_out_b, ln1g, ln1b, w_rt, b_rt, alpha)
        ks_l.append(k.reshape(n_s, 1, 2 * ATT_HEADS, HEAD_DIM))
        vs_l.append(v.reshape(n_s, 1, ATT_HEADS, V_DIM))
        cs_l.append(jnp.concatenate([state_conv[l][:, 1:], u[:, None, :]], axis=1))
        hs = _moe_block(h_rows, info, w_rt, b_rt, wg_b, wu_b, wd_b, ln2g, ln2b, alpha,
                        n_s).reshape(n_s, 1, D_MODEL)
    return (hp, hs, jnp.stack(kp_l), jnp.stack(vp_l), jnp.stack(cp_l),
            jnp.stack(ks_l), jnp.stack(vs_l), jnp.stack(cs_l))
```
